```python
import math
import jax, jax.numpy as jnp
from jax import lax
import numpy as np

D_MODEL = 1024
BATCH = 16
SEQ = 2048
DEPTH = 2
DEC_BATCH = 8
DEC_SEQ = 8192
PAST_LEN = 128

SSD_EXPAND = 2
SSD_D_INNER = SSD_EXPAND * D_MODEL
SSD_HEAD_DIM = 64
SSD_N_HEADS = SSD_D_INNER // SSD_HEAD_DIM
SSD_N_GROUPS = 8
SSD_D_STATE = 128
SSD_CONV_W = 7
SSD_CHUNK = 128
SSD_GN = SSD_N_GROUPS * SSD_D_STATE
SSD_CONV_DIM = SSD_D_INNER + 2 * SSD_GN
SSD_IN_DIM = SSD_D_INNER + SSD_CONV_DIM + 2 * SSD_N_HEADS

GLA_N_HEADS = 4
GLA_KEY_DIM = D_MODEL // 2
GLA_VALUE_DIM = D_MODEL
GLA_HEAD_K = GLA_KEY_DIM // GLA_N_HEADS
GLA_HEAD_V = GLA_VALUE_DIM // GLA_N_HEADS
GLA_GATE_RANK = 16
GLA_GATE_NORMALIZER = 16.0
GLA_CHUNK = 64
GLA_IN_DIM = 2 * GLA_KEY_DIM + 2 * GLA_VALUE_DIM + 2 * GLA_GATE_RANK

D_FF = 4 * D_MODEL

N_SSD_LAYERS = (DEPTH + 1) // 2
N_GLA_LAYERS = DEPTH // 2
EPS = 1e-5

kernel_name = "hybrid_bidir_ssd_gla_encoder"


def rmsnorm(x, g):
    xf = x.astype(jnp.float32)
    y = xf * lax.rsqrt(jnp.mean(xf * xf, axis=-1, keepdims=True) + EPS)
    return (y * g.astype(jnp.float32)).astype(x.dtype)


def flip_seq(t):
    return jnp.flip(t, axis=1)


def centred_depthwise_conv(u, w, b):
    pad = (SSD_CONV_W - 1) // 2
    y = lax.conv_general_dilated(
        u, w[:, None, :].astype(u.dtype), window_strides=(1,), padding=[(pad, pad)],
        dimension_numbers=("NWC", "WIO", "NWC"), feature_group_count=u.shape[-1])
    return y + b.astype(u.dtype)


def ssd_scan(x, dt, a, bm, cm):
    bsz, l, h, p = x.shape
    g, n = bm.shape[-2:]
    r = h // g
    c = l // SSD_CHUNK
    q = SSD_CHUNK
    xc = (x * dt[..., None]).reshape(bsz, c, q, g, r, p)
    log_a = (dt * a).reshape(bsz, c, q, g, r)
    a_cs = jnp.cumsum(log_a, axis=2)
    bc = bm.reshape(bsz, c, q, g, n)
    cc = cm.reshape(bsz, c, q, g, n)
    seg = a_cs[:, :, :, None] - a_cs[:, :, None, :]
    mask = jnp.tril(jnp.ones((q, q), dtype=bool))[None, None, :, :, None, None]
    decay = jnp.exp(jnp.where(mask, seg, -jnp.inf))
    scores = jnp.einsum("bclgn,bcsgn->bclsg", cc, bc)
    y_diag = jnp.einsum("bclsg,bclsgr,bcsgrp->bclgrp", scores, decay, xc)
    decay_to_end = jnp.exp(a_cs[:, :, -1:] - a_cs)
    states = jnp.einsum("bcsgn,bcsgr,bcsgrp->bcgrpn", bc, decay_to_end, xc)
    chunk_decay = jnp.exp(a_cs[:, :, -1])

    def step(carry, inp):
        s, d = inp
        return carry * d[..., None, None] + s, carry

    init = jnp.zeros((bsz, g, r, p, n), jnp.float32)
    _, prev = lax.scan(step, init, (jnp.moveaxis(states, 1, 0), jnp.moveaxis(chunk_decay, 1, 0)))
    prev = jnp.moveaxis(prev, 0, 1)
    y_off = jnp.einsum("bclgn,bcgrpn,bclgr->bclgrp", cc, prev, jnp.exp(a_cs))
    return (y_diag + y_off).reshape(bsz, l, h, p)


def ssd_mixer(u, w_in, conv_w, conv_b, dt_bias_f, dt_bias_b, a_log_f, a_log_b, d_skip, norm_g, w_out):
    bsz, l, _ = u.shape
    f32 = jnp.float32
    proj = u @ w_in
    z = proj[..., :SSD_D_INNER]
    xbc = proj[..., SSD_D_INNER:SSD_D_INNER + SSD_CONV_DIM]
    dt_raw = proj[..., SSD_D_INNER + SSD_CONV_DIM:].astype(f32)
    xbc = jax.nn.silu(centred_depthwise_conv(xbc, conv_w, conv_b)).astype(f32)
    xs = xbc[..., :SSD_D_INNER].reshape(bsz, l, SSD_N_HEADS, SSD_HEAD_DIM)
    bm = xbc[..., SSD_D_INNER:SSD_D_INNER + SSD_GN].reshape(bsz, l, SSD_N_GROUPS, SSD_D_STATE)
    cm = xbc[..., SSD_D_INNER + SSD_GN:].reshape(bsz, l, SSD_N_GROUPS, SSD_D_STATE)
    dt_f = jax.nn.softplus(dt_raw[..., :SSD_N_HEADS] + dt_bias_f.astype(f32))
    dt_b = jax.nn.softplus(dt_raw[..., SSD_N_HEADS:] + dt_bias_b.astype(f32))
    a_f = -jnp.exp(a_log_f.astype(f32))
    a_b = -jnp.exp(a_log_b.astype(f32))
    y_f = ssd_scan(xs, dt_f, a_f, bm, cm)
    y_b = flip_seq(ssd_scan(flip_seq(xs), flip_seq(dt_b), a_b, flip_seq(bm), flip_seq(cm)))
    y = y_f + y_b + d_skip.astype(f32)[:, None] * xs
    y = y.reshape(bsz, l, SSD_D_INNER) * jax.nn.silu(z.astype(f32))
    y = rmsnorm(y, norm_g).astype(u.dtype)
    return y @ w_out


def gla_scan(q, k, v, lg):
    bsz, l, h, dk = q.shape
    dv = v.shape[-1]
    c = l // GLA_CHUNK
    Q = GLA_CHUNK
    q = q.reshape(bsz, c, Q, h, dk)
    k = k.reshape(bsz, c, Q, h, dk)
    v = v.reshape(bsz, c, Q, h, dv)
    bcs = jnp.cumsum(lg.reshape(bsz, c, Q, h, dk), axis=2)
    ref = bcs[:, :, Q // 2:Q // 2 + 1]
    q_in = q * jnp.exp(bcs - ref)
    k_in = k * jnp.exp(ref - bcs)
    attn = jnp.einsum("bclhd,bcshd->bchls", q_in, k_in)
    mask = jnp.tril(jnp.ones((Q, Q), dtype=bool))
    attn = jnp.where(mask, attn, 0.0)
    o_intra = jnp.einsum("bchls,bcshv->bclhv", attn, v)
    b_last = bcs[:, :, -1:]
    k_end = k * jnp.exp(b_last - bcs)
    chunk_states = jnp.einsum("bcshd,bcshv->bchdv", k_end, v)
    chunk_decay = jnp.exp(b_last[:, :, 0])

    def step(carry, inp):
        s, d = inp
        return carry * d[..., None] + s, carry

    init = jnp.zeros((bsz, h, dk, dv), jnp.float32)
    _, prev = lax.scan(step, init, (jnp.moveaxis(chunk_states, 1, 0), jnp.moveaxis(chunk_decay, 1, 0)))
    prev = jnp.moveaxis(prev, 0, 1)
    o_inter = jnp.einsum("bclhd,bchdv->bclhv", q * jnp.exp(bcs), prev)
    return (o_intra + o_inter).reshape(bsz, l, h, dv)


def gla_mixer(u, w_in, gate_up_f, gate_bias_f, gate_up_b, gate_bias_b, norm_g, w_out):
    bsz, l, _ = u.shape
    f32 = jnp.float32
    proj = u @ w_in
    o1 = GLA_KEY_DIM
    o2 = o1 + GLA_KEY_DIM
    o3 = o2 + GLA_VALUE_DIM
    o4 = o3 + GLA_VALUE_DIM
    o5 = o4 + GLA_GATE_RANK
    q = proj[..., :o1].astype(f32).reshape(bsz, l, GLA_N_HEADS, GLA_HEAD_K) * (GLA_HEAD_K ** -0.5)
    k = proj[..., o1:o2].astype(f32).reshape(bsz, l, GLA_N_HEADS, GLA_HEAD_K)
    v = proj[..., o2:o3].astype(f32).reshape(bsz, l, GLA_N_HEADS, GLA_HEAD_V)
    g_out = proj[..., o3:o4].astype(f32)
    lr_f = proj[..., o4:o5]
    lr_b = proj[..., o5:]
    lg_f = (jax.nn.log_sigmoid((lr_f @ gate_up_f + gate_bias_f).astype(f32)) / GLA_GATE_NORMALIZER
            ).reshape(bsz, l, GLA_N_HEADS, GLA_HEAD_K)
    lg_b = (jax.nn.log_sigmoid((lr_b @ gate_up_b + gate_bias_b).astype(f32)) / GLA_GATE_NORMALIZER
            ).reshape(bsz, l, GLA_N_HEADS, GLA_HEAD_K)
    o_f = gla_scan(q, k, v, lg_f)
    o_b = flip_seq(gla_scan(flip_seq(q), flip_seq(k), flip_seq(v), flip_seq(lg_b)))
    o = rmsnorm(o_f + o_b, norm_g)
    o = o.reshape(bsz, l, GLA_VALUE_DIM) * jax.nn.silu(g_out)
    return o.astype(u.dtype) @ w_out


def sqrelu_mlp(u, w_up, w_down):
    h = jnp.square(jax.nn.relu(u @ w_up))
    return h @ w_down


def trunk(x, norm_mix_g, norm_mlp_g, norm_final_g,
          ssd_w_in, ssd_conv_w, ssd_conv_b, ssd_dt_bias_f, ssd_dt_bias_b,
          ssd_a_log_f, ssd_a_log_b, ssd_d, ssd_norm_g, ssd_w_out,
          gla_w_in, gla_gate_up_f, gla_gate_bias_f, gla_gate_up_b, gla_gate_bias_b,
          gla_norm_g, gla_w_out, mlp_w_up, mlp_w_down):
    for i in range(DEPTH):
        u = rmsnorm(x, norm_mix_g[i])
        j = i // 2
        if i % 2 == 0:
            x = x + ssd_mixer(u, ssd_w_in[j], ssd_conv_w[j], ssd_conv_b[j], ssd_dt_bias_f[j],
                              ssd_dt_bias_b[j], ssd_a_log_f[j], ssd_a_log_b[j], ssd_d[j],
                              ssd_norm_g[j], ssd_w_out[j])
        else:
            x = x + gla_mixer(u, gla_w_in[j], gla_gate_up_f[j], gla_gate_bias_f[j],
                              gla_gate_up_b[j], gla_gate_bias_b[j], gla_norm_g[j], gla_w_out[j])
        x = x + sqrelu_mlp(rmsnorm(x, norm_mlp_g[i]), mlp_w_up[i], mlp_w_down[i])
    return rmsnorm(x, norm_final_g)


def _dt_bias(key, shape):
    lo, hi = math.log(1e-3), math.log(1e-1)
    dt = jnp.exp(jax.random.uniform(key, shape, jnp.float32) * (hi - lo) + lo)
    dt = jnp.maximum(dt, 1e-4)
    return dt + jnp.log(-jnp.expm1(-dt))


def setup_inputs(seed: int = 0) -> dict:
    key = jax.random.key(seed)
    ks = jax.random.split(key, 32)
    f32 = jnp.float32
    nrm = lambda k, s, scale: jax.random.normal(k, s, f32) * scale
    NS, NG = N_SSD_LAYERS, N_GLA_LAYERS
    return {
        "x_prompt": nrm(ks[0], (BATCH, SEQ, D_MODEL), 1.0),
        "x_sample": nrm(ks[1], (DEC_BATCH, DEC_SEQ, D_MODEL), 1.0),
        "norm_mix_g": 1.0 + nrm(ks[2], (DEPTH, D_MODEL), 0.02),
        "norm_mlp_g": 1.0 + nrm(ks[3], (DEPTH, D_MODEL), 0.02),
        "norm_final_g": 1.0 + nrm(ks[4], (D_MODEL,), 0.02),
        "ssd_w_in": nrm(ks[5], (NS, D_MODEL, SSD_IN_DIM), D_MODEL ** -0.5),
        "ssd_conv_w": nrm(ks[6], (NS, SSD_CONV_W, SSD_CONV_DIM), SSD_CONV_W ** -0.5),
        "ssd_conv_b": nrm(ks[7], (NS, SSD_CONV_DIM), 0.02),
        "ssd_dt_bias_f": _dt_bias(ks[8], (NS, SSD_N_HEADS)),
        "ssd_dt_bias_b": _dt_bias(ks[9], (NS, SSD_N_HEADS)),
        "ssd_a_log_f": jnp.log(jax.random.uniform(ks[10], (NS, SSD_N_HEADS), f32, 1.0, 16.0)),
        "ssd_a_log_b": jnp.log(jax.random.uniform(ks[11], (NS, SSD_N_HEADS), f32, 1.0, 16.0)),
        "ssd_d": 1.0 + nrm(ks[12], (NS, SSD_N_HEADS), 0.1),
        "ssd_norm_g": 1.0 + nrm(ks[13], (NS, SSD_D_INNER), 0.02),
        "ssd_w_out": nrm(ks[14], (NS, SSD_D_INNER, D_MODEL), SSD_D_INNER ** -0.5),
        "gla_w_in": nrm(ks[15], (NG, D_MODEL, GLA_IN_DIM), D_MODEL ** -0.5),
        "gla_gate_up_f": nrm(ks[16], (NG, GLA_GATE_RANK, GLA_KEY_DIM), GLA_GATE_RANK ** -0.5),
        "gla_gate_bias_f": nrm(ks[17], (NG, GLA_KEY_DIM), 0.1),
        "gla_gate_up_b": nrm(ks[18], (NG, GLA_GATE_RANK, GLA_KEY_DIM), GLA_GATE_RANK ** -0.5),
        "gla_gate_bias_b": nrm(ks[19], (NG, GLA_KEY_DIM), 0.1),
        "gla_norm_g": 1.0 + nrm(ks[20], (NG, GLA_HEAD_V), 0.02),
        "gla_w_out": nrm(ks[21], (NG, GLA_VALUE_DIM, D_MODEL), GLA_VALUE_DIM ** -0.5),
        "mlp_w_up": nrm(ks[22], (DEPTH, D_MODEL, D_FF), D_MODEL ** -0.5),
        "mlp_w_down": nrm(ks[23], (DEPTH, D_FF, D_MODEL), D_FF ** -0.5),
    }


def reference(x_prompt, x_sample, norm_mix_g, norm_mlp_g, norm_final_g,
              ssd_w_in, ssd_conv_w, ssd_conv_b, ssd_dt_bias_f, ssd_dt_bias_b,
              ssd_a_log_f, ssd_a_log_b, ssd_d, ssd_norm_g, ssd_w_out,
              gla_w_in, gla_gate_up_f, gla_gate_bias_f, gla_gate_up_b, gla_gate_bias_b,
              gla_norm_g, gla_w_out, mlp_w_up, mlp_w_down):
    y_prompt = trunk(x_prompt, norm_mix_g, norm_mlp_g, norm_final_g,
                     ssd_w_in, ssd_conv_w, ssd_conv_b, ssd_dt_bias_f, ssd_dt_bias_b,
                     ssd_a_log_f, ssd_a_log_b, ssd_d, ssd_norm_g, ssd_w_out,
                     gla_w_in, gla_gate_up_f, gla_gate_bias_f, gla_gate_up_b, gla_gate_bias_b,
                     gla_norm_g, gla_w_out, mlp_w_up, mlp_w_down)
    y_sample = trunk(x_sample, norm_mix_g, norm_mlp_g, norm_final_g,
                     ssd_w_in, ssd_conv_w, ssd_conv_b, ssd_dt_bias_f, ssd_dt_bias_b,
                     ssd_a_log_f, ssd_a_log_b, ssd_d, ssd_norm_g, ssd_w_out,
                     gla_w_in, gla_gate_up_f, gla_gate_bias_f, gla_gate_up_b, gla_gate_bias_b,
                     gla_norm_g, gla_w_out, mlp_w_up, mlp_w_down)
    return (y_prompt, y_sample)
```

```python
import functools

import jax
import jax.numpy as jnp
from jax import lax
from jax.experimental import pallas as pl
from jax.experimental.pallas import tpu as pltpu

F32 = jnp.float32
BF16 = jnp.bfloat16
EPS = 1e-5
LANES = 128

SSD_H, SSD_P, SSD_G, SSD_N = 32, 64, 8, 128
SSD_R = SSD_H // SSD_G
SSD_DI = SSD_H * SSD_P
SSD_GN = SSD_G * SSD_N
SSD_CONV_W = 7
SSD_Q = 128
GLA_H, GLA_DK, GLA_DV, GLA_RANK = 4, 128, 256, 16
GLA_KD = GLA_H * GLA_DK
GLA_VD = GLA_H * GLA_DV
GLA_Q = 64
GLA_GATE_NORMALIZER = 16.0

HALO = 16
CONV_RB, CONV_CB = 32, 512
VMEM_LIMIT = 56 * 1024 * 1024


def _dot(a, b):
    return jnp.dot(a, b, preferred_element_type=F32)


def _dot_nt(a, b):
    return lax.dot_general(a, b, (((1,), (1,)), ((), ())), preferred_element_type=F32)


def _dot_tn(a, b):
    return lax.dot_general(a, b, (((0,), (0,)), ((), ())), preferred_element_type=F32)


def _rms(x, g):
    return x * lax.rsqrt(jnp.mean(x * x, axis=-1, keepdims=True) + EPS) * g


def _silu(x):
    return x * (1.0 / (1.0 + jnp.exp(-x)))


def _softplus(x):
    return jnp.maximum(x, 0.0) + jnp.log1p(jnp.exp(-jnp.abs(x)))


def _split(x, parts):
    out = []
    for _ in range(parts - 1):
        p = x.astype(BF16)
        out.append(p)
        x = x - p.astype(F32)
    out.append(x.astype(BF16))
    return out


def _tri(q, upper):
    row = lax.broadcasted_iota(jnp.int32, (q, q), 0)
    col = lax.broadcasted_iota(jnp.int32, (q, q), 1)
    return (col >= row) if upper else (row >= col)


def _cumsum(tri_bf16, x):
    t3 = jnp.concatenate([tri_bf16] * 3, axis=1)
    x3 = jnp.concatenate(_split(x, 3), axis=0)
    return _dot(t3, x3)


def _expand(x, sel2_ref):
    hi, lo = _split(x, 2)
    return _dot(jnp.concatenate([hi, lo], axis=1), sel2_ref[...])


def _ssd_in_kernel(xp_ref, x_ref, xn_ref, g_ref, wz_ref, wx_ref, wdt_ref, cw_ref, cb_ref,
                   z_ref, xbc_ref, dt_ref, xn_scr, p_scr, *, tm, nt):
    t = pl.program_id(1)
    g = g_ref[...]
    xn_scr[0:HALO, :] = _rms(xp_ref[0], g).astype(BF16)
    xn_scr[HALO:HALO + tm, :] = _rms(x_ref[0], g).astype(BF16)
    xn_scr[HALO + tm:, :] = _rms(xn_ref[0], g).astype(BF16)
    xm = xn_scr[HALO:HALO + tm, :]
    z_ref[0] = _dot(xm, wz_ref[...])
    dt_ref[0] = _dot(xm, wdt_ref[...])
    nc = wx_ref.shape[1]
    ncb = nc // CONV_CB

    def proj(j, carry):
        c0 = pl.multiple_of(j * CONV_CB, CONV_CB)
        p_scr[:, pl.ds(c0, CONV_CB)] = _dot(xn_scr[...], wx_ref[:, pl.ds(c0, CONV_CB)])
        return carry

    lax.fori_loop(0, ncb, proj, 0)

    @pl.when(t == 0)
    def _():
        p_scr[0:HALO, :] = jnp.zeros((HALO, nc), F32)

    @pl.when(t == nt - 1)
    def _():
        p_scr[HALO + tm:, :] = jnp.zeros((HALO, nc), F32)

    pad = (SSD_CONV_W - 1) // 2
    off = 8 - pad

    def conv(i, carry):
        rb = i // ncb
        cb = i - rb * ncb
        r0 = pl.multiple_of(rb * CONV_RB, CONV_RB)
        c0 = pl.multiple_of(cb * CONV_CB, CONV_CB)
        win = p_scr[pl.ds(r0 + HALO - 8, CONV_RB + 16), pl.ds(c0, CONV_CB)]
        w = cw_ref[:, pl.ds(c0, CONV_CB)]
        acc = jnp.zeros((CONV_RB, CONV_CB), F32)
        for k in range(SSD_CONV_W):
            acc = acc + win[off + k:off + k + CONV_RB, :] * w[k:k + 1, :]
        acc = acc + cb_ref[:, pl.ds(c0, CONV_CB)]
        xbc_ref[0, pl.ds(r0, CONV_RB), pl.ds(c0, CONV_CB)] = _silu(acc)
        return carry

    lax.fori_loop(0, (tm // CONV_RB) * ncb, conv, 0)


def _ssd_in(x, g, wz, wx, wdt, cw, cb, *, tm):
    b, l, d = x.shape
    nt = l // tm
    hb = tm // HALO
    nz, nx, ndt = wz.shape[1], wx.shape[1], wdt.shape[1]
    const = lambda i, t: (0, 0)
    return pl.pallas_call(
        functools.partial(_ssd_in_kernel, tm=tm, nt=nt),
        grid=(b, nt),
        in_specs=[
            pl.BlockSpec((1, HALO, d), lambda i, t: (i, jnp.maximum(t * hb - 1, 0), 0)),
            pl.BlockSpec((1, tm, d), lambda i, t: (i, t, 0)),
            pl.BlockSpec((1, HALO, d), lambda i, t: (i, jnp.minimum((t + 1) * hb, l // HALO - 1), 0)),
            pl.BlockSpec((1, d), const),
            pl.BlockSpec((d, nz), const),
            pl.BlockSpec((d, nx), const),
            pl.BlockSpec((d, ndt), const),
            pl.BlockSpec((8, nx), const),
            pl.BlockSpec((1, nx), const),
        ],
        out_specs=[
            pl.BlockSpec((1, tm, nz), lambda i, t: (i, t, 0)),
            pl.BlockSpec((1, tm, nx), lambda i, t: (i, t, 0)),
            pl.BlockSpec((1, tm, ndt), lambda i, t: (i, t, 0)),
        ],
        out_shape=[
            jax.ShapeDtypeStruct((b, l, nz), F32),
            jax.ShapeDtypeStruct((b, l, nx), F32),
            jax.ShapeDtypeStruct((b, l, ndt), F32),
        ],
        scratch_shapes=[
            pltpu.VMEM((tm + 2 * HALO, d), BF16),
            pltpu.VMEM((tm + 2 * HALO, nx), F32),
        ],
        compiler_params=pltpu.CompilerParams(
            dimension_semantics=("arbitrary", "arbitrary"), vmem_limit_bytes=VMEM_LIMIT),
        name="ssd_in",
    )(x, x, x, g, wz, wx, wdt, cw, cb)


def _ssd_decay_inputs(dt_ref, bias_ref, alog_ref):
    dtv = _softplus(dt_ref[0] + bias_ref[...])
    la = dtv * (-jnp.exp(alog_ref[...]))
    return dtv, la


def _ssd_bwd_kernel(xs_ref, b_ref, c_ref, dt_ref, bias_ref, alog_ref, selb_ref, yb_ref, s_scr):
    q = SSD_Q

    @pl.when(pl.program_id(1) == 0)
    def _():
        s_scr[...] = jnp.zeros_like(s_scr)

    dtv, la = _ssd_decay_inputs(dt_ref, bias_ref, alog_ref)
    rcs = _cumsum(_tri(q, True).astype(BF16), la)
    e = jnp.exp(rcs)
    w = dtv * jnp.exp(rcs[0:1, :] - rcs)
    eexp = _expand(e, selb_ref)
    xw = (xs_ref[0] * _expand(w, selb_ref)).astype(BF16)
    gw = SSD_R * SSD_P
    for g in range(SSD_G):
        ns = slice(g * SSD_N, (g + 1) * SSD_N)
        ps = slice(g * gw, (g + 1) * gw)
        bg = b_ref[0, :, ns].astype(BF16)
        cg = c_ref[0, :, ns].astype(BF16)
        s_in = s_scr[g]
        yb_ref[0, :, ps] = _dot(cg, s_in.astype(BF16)) * eexp[:, ps]
        s_scr[g] = s_in * eexp[0:1, ps] + _dot_tn(bg, xw[:, ps])


def _ssd_fwd_kernel(xs_ref, b_ref, c_ref, dt_ref, z_ref, yb_ref, xres_ref, bias_ref, alog_ref, self_ref,
                    dexp_ref, ng_ref, wout_ref, out_ref, s_scr, y_scr):
    q = SSD_Q

    @pl.when(pl.program_id(1) == 0)
    def _():
        s_scr[...] = jnp.zeros_like(s_scr)

    dtv, la = _ssd_decay_inputs(dt_ref, bias_ref, alog_ref)
    lower = _tri(q, False)
    upper = _tri(q, True)
    cs = _cumsum(lower.astype(BF16), la)
    rcs = _cumsum(upper.astype(BF16), la)
    cs_t, rcs_t, dt_t = cs.T, rcs.T, dtv.T
    ml = lower.astype(F32)
    mu = upper.astype(F32)
    lane = lax.broadcasted_iota(jnp.int32, (q, 2 * SSD_P), 1)
    gw = SSD_R * SSD_P
    for g in range(SSD_G):
        ns = slice(g * SSD_N, (g + 1) * SSD_N)
        bg = b_ref[0, :, ns].astype(BF16)
        cg = c_ref[0, :, ns].astype(BF16)
        gm = _dot_nt(cg, bg)
        for pair in range(SSD_R // 2):
            col = g * gw + pair * 2 * SSD_P
            xpair = xs_ref[0, :, col:col + 2 * SSD_P]
            acc = jnp.zeros((q, 2 * SSD_P), F32)
            for half in range(2):
                h = g * SSD_R + pair * 2 + half
                hb = SSD_H + h
                seg = jnp.where(lower, cs[:, h:h + 1] - cs_t[h:h + 1, :], rcs[:, hb:hb + 1] - rcs_t[hb:hb + 1, :])
                wt = ml * dt_t[h:h + 1, :] + mu * dt_t[hb:hb + 1, :]
                m = (gm * jnp.exp(seg) * wt).astype(BF16)
                keep = (lane < SSD_P) if half == 0 else (lane >= SSD_P)
                acc = acc + _dot(m, jnp.where(keep, xpair, 0.0).astype(BF16))
            y_scr[:, col:col + 2 * SSD_P] = acc
    e = jnp.exp(cs)
    w = dtv * jnp.exp(cs[q - 1:q, :] - cs)
    eexp = _expand(e, self_ref)
    xs = xs_ref[0]
    xw = (xs * _expand(w, self_ref)).astype(BF16)
    for g in range(SSD_G):
        ns = slice(g * SSD_N, (g + 1) * SSD_N)
        ps = slice(g * gw, (g + 1) * gw)
        bg = b_ref[0, :, ns].astype(BF16)
        cg = c_ref[0, :, ns].astype(BF16)
        s_in = s_scr[g]
        y_scr[:, ps] = y_scr[:, ps] + _dot(cg, s_in.astype(BF16)) * eexp[:, ps]
        s_scr[g] = s_in * eexp[q - 1:q, ps] + _dot_tn(bg, xw[:, ps])
    y = y_scr[...] + yb_ref[0] + dexp_ref[...] * xs
    y = y * _silu(z_ref[0])
    yn = _rms(y, ng_ref[...]).astype(BF16)
    out_ref[0] = xres_ref[0] + _dot(yn, wout_ref[...])


def _ssd_sweeps(xbc, dt, z, xres, bias, alog, sel_f, sel_b, dexp, ng, wout):
    b, l, _ = xbc.shape
    q = SSD_Q
    nc = l // q
    d = xres.shape[-1]
    const = lambda i, c: (0, 0)
    cp = pltpu.CompilerParams(dimension_semantics=("arbitrary", "arbitrary"), vmem_limit_bytes=VMEM_LIMIT)
    nb = SSD_DI // SSD_GN

    def chunk_specs(cmap):
        return [
            pl.BlockSpec((1, q, SSD_DI), lambda i, c: (i, cmap(c), 0)),
            pl.BlockSpec((1, q, SSD_GN), lambda i, c: (i, cmap(c), nb)),
            pl.BlockSpec((1, q, SSD_GN), lambda i, c: (i, cmap(c), nb + 1)),
            pl.BlockSpec((1, q, LANES), lambda i, c: (i, cmap(c), 0)),
        ]

    rev = lambda c: nc - 1 - c
    fwd = lambda c: c
    yb = pl.pallas_call(
        _ssd_bwd_kernel,
        grid=(b, nc),
        in_specs=chunk_specs(rev) + [
            pl.BlockSpec((1, LANES), const),
            pl.BlockSpec((1, LANES), const),
            pl.BlockSpec((2 * LANES, SSD_DI), const),
        ],
        out_specs=pl.BlockSpec((1, q, SSD_DI), lambda i, c: (i, rev(c), 0)),
        out_shape=jax.ShapeDtypeStruct((b, l, SSD_DI), F32),
        scratch_shapes=[pltpu.VMEM((SSD_G, SSD_N, SSD_R * SSD_P), F32)],
        compiler_params=cp,
        name="ssd_bwd",
    )(xbc, xbc, xbc, dt, bias, alog, sel_b)
    return pl.pallas_call(
        _ssd_fwd_kernel,
        grid=(b, nc),
        in_specs=chunk_specs(fwd) + [
            pl.BlockSpec((1, q, SSD_DI), lambda i, c: (i, c, 0)),
            pl.BlockSpec((1, q, SSD_DI), lambda i, c: (i, c, 0)),
            pl.BlockSpec((1, q, d), lambda i, c: (i, c, 0)),
            pl.BlockSpec((1, LANES), const),
            pl.BlockSpec((1, LANES), const),
            pl.BlockSpec((2 * LANES, SSD_DI), const),
            pl.BlockSpec((1, SSD_DI), const),
            pl.BlockSpec((1, SSD_DI), const),
            pl.BlockSpec((SSD_DI, d), const),
        ],
        out_specs=pl.BlockSpec((1, q, d), lambda i, c: (i, c, 0)),
        out_shape=jax.ShapeDtypeStruct((b, l, d), F32),
        scratch_shapes=[pltpu.VMEM((SSD_G, SSD_N, SSD_R * SSD_P), F32), pltpu.VMEM((q, SSD_DI), F32)],
        compiler_params=cp,
        name="ssd_fwd",
    )(xbc, xbc, xbc, dt, z, yb, xres, bias, alog, sel_f, dexp, ng, wout)


def _gla_in_kernel(x_ref, g_ref, wqk_ref, wv_ref, wg_ref, wlr_ref, qk_ref, v_ref, go_ref, lr_ref):
    xn = _rms(x_ref[...], g_ref[...]).astype(BF16)
    qk_ref[...] = _dot(xn, wqk_ref[...])
    v_ref[...] = _dot(xn, wv_ref[...])
    go_ref[...] = _dot(xn, wg_ref[...])
    lr_ref[...] = _dot(xn, wlr_ref[...])


def _gla_in(x2, g, wqk, wv, wg, wlr, *, tm):
    t, d = x2.shape
    const = lambda i: (0, 0)
    widths = [wqk.shape[1], wv.shape[1], wg.shape[1], wlr.shape[1]]
    return pl.pallas_call(
        _gla_in_kernel,
        grid=(t // tm,),
        in_specs=[pl.BlockSpec((tm, d), lambda i: (i, 0)), pl.BlockSpec((1, d), const)]
        + [pl.BlockSpec((d, n), const) for n in widths],
        out_specs=[pl.BlockSpec((tm, n), lambda i: (i, 0)) for n in widths],
        out_shape=[jax.ShapeDtypeStruct((t, n), F32) for n in widths],
        compiler_params=pltpu.CompilerParams(dimension_semantics=("arbitrary",), vmem_limit_bytes=VMEM_LIMIT),
        name="gla_in",
    )(x2, g, wqk, wv, wg, wlr)


def _gla_log_gate(lr_bf16, gup_ref, gbias_ref):
    xg = _dot(lr_bf16, gup_ref[...]) + gbias_ref[...]
    return -_softplus(-xg) * (1.0 / GLA_GATE_NORMALIZER)


def _gla_bwd_kernel(qk_ref, v_ref, lr_ref, gup_ref, gbias_ref, ob_ref, s_scr):
    q = GLA_Q

    @pl.when(pl.program_id(1) == 0)
    def _():
        s_scr[...] = jnp.zeros_like(s_scr)

    lg = _gla_log_gate(lr_ref[0].astype(BF16), gup_ref, gbias_ref)
    rb = _cumsum(_tri(q, True).astype(BF16), lg)
    qs = qk_ref[0, :, 0:GLA_KD] * (GLA_DK ** -0.5)
    ks = qk_ref[0, :, GLA_KD:2 * GLA_KD]
    qb = (qs * jnp.exp(rb)).astype(BF16)
    kend = (ks * jnp.exp(rb[0:1, :] - rb)).astype(BF16)
    decay = jnp.exp(rb[0:1, :])
    for h in range(GLA_H):
        ksl = slice(h * GLA_DK, (h + 1) * GLA_DK)
        vsl = slice(h * GLA_DV, (h + 1) * GLA_DV)
        s_in = s_scr[h]
        ob_ref[0, :, vsl] = _dot_nt(qb[:, ksl], s_in.astype(BF16))
        s_scr[h] = s_in * decay[:, ksl] + _dot_tn(v_ref[0, :, vsl].astype(BF16), kend[:, ksl])


def _gla_fwd_kernel(qk_ref, v_ref, lr_ref, go_ref, ob_ref, xres_ref, gupf_ref, gbf_ref, gupb_ref, gbb_ref,
                    ng_ref, wout_ref, out_ref, s_scr, o_scr):
    q = GLA_Q

    @pl.when(pl.program_id(1) == 0)
    def _():
        s_scr[...] = jnp.zeros_like(s_scr)

    lr = lr_ref[0].astype(BF16)
    lower = _tri(q, False)
    upper = _tri(q, True)
    bf = _cumsum(lower.astype(BF16), _gla_log_gate(lr, gupf_ref, gbf_ref))
    rb = _cumsum(upper.astype(BF16), _gla_log_gate(lr, gupb_ref, gbb_ref))
    qs = qk_ref[0, :, 0:GLA_KD] * (GLA_DK ** -0.5)
    ks = qk_ref[0, :, GLA_KD:2 * GLA_KD]
    mid_f = bf[q // 2:q // 2 + 1, :]
    mid_b = rb[q - 1 - q // 2:q - q // 2, :]
    qin_f = (qs * jnp.exp(bf - mid_f)).astype(BF16)
    kin_f = (ks * jnp.exp(mid_f - bf)).astype(BF16)
    qin_b = (qs * jnp.exp(rb - mid_b)).astype(BF16)
    kin_b = (ks * jnp.exp(mid_b - rb)).astype(BF16)
    last = bf[q - 1:q, :]
    qb = (qs * jnp.exp(bf)).astype(BF16)
    kend = (ks * jnp.exp(last - bf)).astype(BF16)
    decay = jnp.exp(last)
    for h in range(GLA_H):
        ksl = slice(h * GLA_DK, (h + 1) * GLA_DK)
        vsl = slice(h * GLA_DV, (h + 1) * GLA_DV)
        vh = v_ref[0, :, vsl].astype(BF16)
        attn = (jnp.where(lower, _dot_nt(qin_f[:, ksl], kin_f[:, ksl]), 0.0)
                + jnp.where(upper, _dot_nt(qin_b[:, ksl], kin_b[:, ksl]), 0.0))
        s_in = s_scr[h]
        o = _dot(attn.astype(BF16), vh) + _dot_nt(qb[:, ksl], s_in.astype(BF16)) + ob_ref[0, :, vsl]
        s_scr[h] = s_in * decay[:, ksl] + _dot_tn(vh, kend[:, ksl])
        o_scr[:, vsl] = _rms(o, ng_ref[...])
    o = (o_scr[...] * _silu(go_ref[0])).astype(BF16)
    out_ref[0] = xres_ref[0] + _dot(o, wout_ref[...])


def _gla_sweeps(qk, v, lr, go, xres, gupf, gbf, gupb, gbb, ng, wout):
    b, l, _ = qk.shape
    q = GLA_Q
    nc = l // q
    d = xres.shape[-1]
    const = lambda i, c: (0, 0)
    cp = pltpu.CompilerParams(dimension_semantics=("arbitrary", "arbitrary"), vmem_limit_bytes=VMEM_LIMIT)
    rev = lambda c: nc - 1 - c
    state = pltpu.VMEM((GLA_H, GLA_DV, GLA_DK), F32)
    ob = pl.pallas_call(
        _gla_bwd_kernel,
        grid=(b, nc),
        in_specs=[
            pl.BlockSpec((1, q, 2 * GLA_KD), lambda i, c: (i, rev(c), 0)),
            pl.BlockSpec((1, q, GLA_VD), lambda i, c: (i, rev(c), 0)),
            pl.BlockSpec((1, q, LANES), lambda i, c: (i, rev(c), 0)),
            pl.BlockSpec((LANES, GLA_KD), const),
            pl.BlockSpec((1, GLA_KD), const),
        ],
        out_specs=pl.BlockSpec((1, q, GLA_VD), lambda i, c: (i, rev(c), 0)),
        out_shape=jax.ShapeDtypeStruct((b, l, GLA_VD), F32),
        scratch_shapes=[state],
        compiler_params=cp,
        name="gla_bwd",
    )(qk, v, lr, gupb, gbb)
    tok = lambda n: pl.BlockSpec((1, q, n), lambda i, c: (i, c, 0))
    return pl.pallas_call(
        _gla_fwd_kernel,
        grid=(b, nc),
        in_specs=[tok(2 * GLA_KD), tok(GLA_VD), tok(LANES), tok(GLA_VD), tok(GLA_VD), tok(d),
                  pl.BlockSpec((LANES, GLA_KD), const), pl.BlockSpec((1, GLA_KD), const),
                  pl.BlockSpec((LANES, GLA_KD), const), pl.BlockSpec((1, GLA_KD), const),
                  pl.BlockSpec((1, GLA_DV), const), pl.BlockSpec((GLA_VD, d), const)],
        out_specs=tok(d),
        out_shape=jax.ShapeDtypeStruct((b, l, d), F32),
        scratch_shapes=[state, pltpu.VMEM((q, GLA_VD), F32)],
        compiler_params=cp,
        name="gla_fwd",
    )(qk, v, lr, go, ob, xres, gupf, gbf, gupb, gbb, ng, wout)


def _mlp_kernel(*refs, fc, final):
    if final:
        x_ref, g_ref, wup_ref, wdn_ref, fg_ref, out_ref = refs
    else:
        x_ref, g_ref, wup_ref, wdn_ref, out_ref = refs
    x = x_ref[...]
    xn = _rms(x, g_ref[...]).astype(BF16)
    acc = x
    for j in range(wup_ref.shape[1] // fc):
        h = jnp.square(jnp.maximum(_dot(xn, wup_ref[:, j * fc:(j + 1) * fc]), 0.0)).astype(BF16)
        acc = acc + _dot(h, wdn_ref[j * fc:(j + 1) * fc, :])
    if final:
        acc = _rms(acc, fg_ref[...])
    out_ref[...] = acc


def _mlp(x2, g, wup, wdn, final_g, *, tm, fc):
    t, d = x2.shape
    ff = wup.shape[1]
    const = lambda i: (0, 0)
    final = final_g is not None
    ins = [x2, g, wup, wdn] + ([final_g] if final else [])
    specs = [pl.BlockSpec((tm, d), lambda i: (i, 0)), pl.BlockSpec((1, d), const),
             pl.BlockSpec((d, ff), const), pl.BlockSpec((ff, d), const)] + ([pl.BlockSpec((1, d), const)] if final else [])
    return pl.pallas_call(
        functools.partial(_mlp_kernel, fc=fc, final=final),
        grid=(t // tm,),
        in_specs=specs,
        out_specs=pl.BlockSpec((tm, d), lambda i: (i, 0)),
        out_shape=jax.ShapeDtypeStruct((t, d), F32),
        compiler_params=pltpu.CompilerParams(dimension_semantics=("arbitrary",), vmem_limit_bytes=VMEM_LIMIT),
        name="mlp_final" if final else "mlp",
    )(*ins)


def _pad_cols(w, n):
    return jnp.pad(w, ((0, 0), (0, n - w.shape[1])))


def _row(v, n=None):
    v = v.reshape(1, -1).astype(F32)
    return v if n is None else _pad_cols(v, n)


def _head_selector(first_row):
    r = jnp.arange(LANES)[:, None]
    c = jnp.arange(SSD_DI)[None, :]
    sel = (r == first_row + c // SSD_P).astype(BF16)
    return jnp.concatenate([sel, sel], axis=0)


def _ssd_layer(x, mix_g, p, *, tm):
    w_in = p["w_in"]
    wz = w_in[:, :SSD_DI].astype(BF16)
    wx = w_in[:, SSD_DI:2 * SSD_DI + 2 * SSD_GN].astype(BF16)
    wdt = _pad_cols(w_in[:, 2 * SSD_DI + 2 * SSD_GN:], LANES).astype(BF16)
    cw = jnp.pad(p["conv_w"].astype(F32), ((0, 8 - SSD_CONV_W), (0, 0)))
    z, xbc, dt = _ssd_in(x, _row(mix_g), wz, wx, wdt, cw, _row(p["conv_b"]), tm=tm)
    bias = _row(jnp.concatenate([p["dt_bias_f"], p["dt_bias_b"]]), LANES)
    alog = _row(jnp.concatenate([p["a_log_f"], p["a_log_b"]]), LANES)
    dexp = _row(jnp.repeat(p["d"], SSD_P))
    return _ssd_sweeps(xbc, dt, z, x, bias, alog, _head_selector(0), _head_selector(SSD_H), dexp,
                       _row(p["norm_g"]), p["w_out"].astype(BF16))


def _gla_layer(x, mix_g, p, *, tm):
    b, l, d = x.shape
    w_in = p["w_in"]
    o1, o2, o3 = 2 * GLA_KD, 2 * GLA_KD + GLA_VD, 2 * GLA_KD + 2 * GLA_VD
    qk, v, go, lr = _gla_in(x.reshape(b * l, d), _row(mix_g), w_in[:, :o1].astype(BF16), w_in[:, o1:o2].astype(BF16),
                            w_in[:, o2:o3].astype(BF16), _pad_cols(w_in[:, o3:], LANES).astype(BF16), tm=tm)
    pad_rows = lambda w, first: jnp.pad(w, ((first, LANES - first - GLA_RANK), (0, 0))).astype(BF16)
    sh = lambda a: a.reshape(b, l, a.shape[-1])
    return _gla_sweeps(sh(qk), sh(v), sh(lr), sh(go), x,
                       pad_rows(p["gate_up_f"], 0), _row(p["gate_bias_f"]),
                       pad_rows(p["gate_up_b"], GLA_RANK), _row(p["gate_bias_b"]),
                       _row(p["norm_g"]), p["w_out"].astype(BF16))


def _trunk(x, norm_mix_g, norm_mlp_g, norm_final_g, ssd, gla, mlp_w_up, mlp_w_down):
    b, l, d = x.shape
    depth = norm_mix_g.shape[0]
    tm = min(512, l)
    for i in range(depth):
        j = i // 2
        if i % 2 == 0:
            x = _ssd_layer(x, norm_mix_g[i], {k: v[j] for k, v in ssd.items()}, tm=tm)
        else:
            x = _gla_layer(x, norm_mix_g[i], {k: v[j] for k, v in gla.items()}, tm=tm)
        final_g = _row(norm_final_g) if i == depth - 1 else None
        x = _mlp(x.reshape(b * l, d), _row(norm_mlp_g[i]), mlp_w_up[i].astype(BF16), mlp_w_down[i].astype(BF16),
                 final_g, tm=tm, fc=1024).reshape(b, l, d)
    return x


def kernel(x_prompt, x_sample, norm_mix_g, norm_mlp_g, norm_final_g, ssd_w_in, ssd_conv_w, ssd_conv_b, ssd_dt_bias_f, ssd_dt_bias_b, ssd_a_log_f, ssd_a_log_b, ssd_d, ssd_norm_g, ssd_w_out, gla_w_in, gla_gate_up_f, gla_gate_bias_f, gla_gate_up_b, gla_gate_bias_b, gla_norm_g, gla_w_out, mlp_w_up, mlp_w_down):
    ssd = dict(w_in=ssd_w_in, conv_w=ssd_conv_w, conv_b=ssd_conv_b, dt_bias_f=ssd_dt_bias_f, dt_bias_b=ssd_dt_bias_b,
               a_log_f=ssd_a_log_f, a_log_b=ssd_a_log_b, d=ssd_d, norm_g=ssd_norm_g, w_out=ssd_w_out)
    gla = dict(w_in=gla_w_in, gate_up_f=gla_gate_up_f, gate_bias_f=gla_gate_bias_f, gate_up_b=gla_gate_up_b,
               gate_bias_b=gla_gate_bias_b, norm_g=gla_norm_g, w_out=gla_w_out)
    run = functools.partial(_trunk, norm_mix_g=norm_mix_g, norm_mlp_g=norm_mlp_g, norm_final_g=norm_final_g,
                            ssd=ssd, gla=gla, mlp_w_up=mlp_w_up, mlp_w_down=mlp_w_down)
    return (run(x_prompt), run(x_sample))
```

```python
import functools

import jax
import jax.numpy as jnp
from jax import lax
from jax.experimental import pallas as pl
from jax.experimental.pallas import tpu as pltpu

F32 = jnp.float32
BF16 = jnp.bfloat16
EPS = 1e-5
LANES = 128

SSD_H, SSD_P, SSD_G, SSD_N = 32, 64, 8, 128
SSD_R = SSD_H // SSD_G
SSD_DI = SSD_H * SSD_P
SSD_GN = SSD_G * SSD_N
SSD_CONV_W = 7
SSD_Q = 128
GLA_H, GLA_DK, GLA_DV, GLA_RANK = 4, 128, 256, 16
GLA_KD = GLA_H * GLA_DK
GLA_VD = GLA_H * GLA_DV
GLA_Q = 64
GLA_GATE_NORMALIZER = 16.0
SSD_TB = 256
GLA_TB = 256

SUBLANES = 8
HALO = 16
CONV_PAD_TILES = 4
CONV_CB = 512
VMEM_LIMIT = 56 * 1024 * 1024


def _dot(a, b):
    return jnp.dot(a, b, preferred_element_type=F32)


def _dot_nt(a, b):
    return lax.dot_general(a, b, (((1,), (1,)), ((), ())), preferred_element_type=F32)


def _dot_tn(a, b):
    return lax.dot_general(a, b, (((0,), (0,)), ((), ())), preferred_element_type=F32)


def _rms(x, g):
    return x * lax.rsqrt(jnp.mean(x * x, axis=-1, keepdims=True) + EPS) * g


def _silu(x):
    return x * (1.0 / (1.0 + jnp.exp(-x)))


def _softplus(x):
    return jnp.maximum(x, 0.0) + jnp.log1p(jnp.exp(-jnp.abs(x)))


def _split(x, parts):
    out = []
    for _ in range(parts - 1):
        p = x.astype(BF16)
        out.append(p)
        x = x - p.astype(F32)
    out.append(x.astype(BF16))
    return out


def _tri(q, upper):
    row = lax.broadcasted_iota(jnp.int32, (q, q), 0)
    col = lax.broadcasted_iota(jnp.int32, (q, q), 1)
    return (col >= row) if upper else (row >= col)


def _tri_blocks(tb, q, upper):
    sh = q.bit_length() - 1
    row = lax.broadcasted_iota(jnp.int32, (tb, tb), 0)
    col = lax.broadcasted_iota(jnp.int32, (tb, tb), 1)
    same = lax.shift_right_logical(row, sh) == lax.shift_right_logical(col, sh)
    return same & ((col >= row) if upper else (row >= col))


def _cumsum(tri_bf16, x):
    t3 = jnp.concatenate([tri_bf16] * 3, axis=1)
    x3 = jnp.concatenate(_split(x, 3), axis=0)
    return _dot(t3, x3)


def _expand(x, sel2_ref):
    hi, lo = _split(x, 2)
    return _dot(jnp.concatenate([hi, lo], axis=1), sel2_ref[...])


def _ssd_in_kernel(xp_ref, x_ref, xn_ref, g_ref, wz_ref, wx_ref, wdt_ref, cw_ref, cb_ref,
                   z_ref, xbc_ref, dt_ref, xn_scr, xnf_scr, lhs_scr, p_scr, u_scr, *, tm, nt):
    t = pl.program_id(1)
    g = g_ref[...]
    seg = tm // SUBLANES
    pitch = seg + SUBLANES
    nslab = x_ref.shape[2] // LANES
    njj = seg + 2 * CONV_PAD_TILES
    ncb = wx_ref.shape[1] // CONV_CB
    nlt = CONV_CB // LANES

    def put_f32(row0, xn):
        for c in range(nslab):
            xnf_scr[c, row0:row0 + xn.shape[0], :] = xn[:, c * LANES:(c + 1) * LANES]

    edge = _rms(xp_ref[0, HALO - SUBLANES:HALO, :], g)
    put_f32(seg - SUBLANES, jnp.where(t == 0, 0.0, edge))
    edge = _rms(xn_ref[0, 0:SUBLANES, :], g)
    put_f32((SUBLANES + 1) * pitch, jnp.where(t == nt - 1, 0.0, edge))
    for s in range(SUBLANES):
        xn = _rms(x_ref[0, s * seg:(s + 1) * seg, :], g)
        xn_scr[s * seg:(s + 1) * seg, :] = xn.astype(BF16)
        put_f32((s + 1) * pitch, xn)

    for c in range(nslab):
        for jp in range(njj // 2):
            tiles = []
            for jj in (2 * jp, 2 * jp + 1):
                off = jj - CONV_PAD_TILES
                start = pitch + off if off >= 0 else off + seg
                start = start if off < seg else 2 * pitch + off - seg
                tiles.append(xnf_scr[c, pl.ds(start, SUBLANES, stride=pitch), :])
            lhs_scr[jp * 2 * SUBLANES:(jp + 1) * 2 * SUBLANES, c * LANES:(c + 1) * LANES] = (
                jnp.concatenate(tiles, axis=0).astype(BF16))

    xm = xn_scr[...]
    z_ref[0] = _dot(xm, wz_ref[...])
    dt_ref[0] = _dot(xm, wdt_ref[...])

    def project(j, slot):
        c0 = pl.multiple_of(j * CONV_CB, CONV_CB)
        p_scr[slot] = _dot(lhs_scr[...], wx_ref[:, pl.ds(c0, CONV_CB)])

    def conv(j, slot):
        c0 = pl.multiple_of(j * CONV_CB, CONV_CB)
        w_all = cw_ref[:, pl.ds(c0, CONV_CB)]
        b_all = cb_ref[:, pl.ds(c0, CONV_CB)]
        for lt in range(nlt):
            ls = slice(lt * LANES, (lt + 1) * LANES)
            w = [jnp.broadcast_to(w_all[k:k + 1, ls], (SUBLANES, LANES)) for k in range(SSD_CONV_W)]
            bias = jnp.broadcast_to(b_all[:, ls], (SUBLANES, LANES))
            taps = [p_scr[slot, jj * SUBLANES:(jj + 1) * SUBLANES, ls] for jj in range(njj)]
            first = CONV_PAD_TILES - (SSD_CONV_W - 1) // 2
            for j2 in range(seg):
                acc = bias
                for k in range(SSD_CONV_W):
                    acc = acc + taps[j2 + first + k] * w[k]
                u_scr[lt, j2 * SUBLANES:(j2 + 1) * SUBLANES, :] = _silu(acc)
            for s in range(SUBLANES):
                for a in range(seg // SUBLANES):
                    r0 = s * seg + a * SUBLANES
                    xbc_ref[0, r0:r0 + SUBLANES, pl.ds(pl.multiple_of(c0 + lt * LANES, LANES), LANES)] = (
                        u_scr[lt, pl.ds(a * SUBLANES * SUBLANES + s, SUBLANES, stride=SUBLANES), :])

    project(0, 0)

    def body(i, carry):
        project(2 * i + 1, 1)
        conv(2 * i, 0)
        project(2 * i + 2, 0)
        conv(2 * i + 1, 1)
        return carry

    lax.fori_loop(0, ncb // 2 - 1, body, 0)
    project(ncb - 1, 1)
    conv(ncb - 2, 0)
    conv(ncb - 1, 1)


def _ssd_in(x, g, wz, wx, wdt, cw, cb, *, tm):
    b, l, d = x.shape
    nt = l // tm
    hb = tm // HALO
    nz, nx, ndt = wz.shape[1], wx.shape[1], wdt.shape[1]
    const = lambda i, t: (0, 0)
    seg = tm // SUBLANES
    return pl.pallas_call(
        functools.partial(_ssd_in_kernel, tm=tm, nt=nt),
        grid=(b, nt),
        in_specs=[
            pl.BlockSpec((1, HALO, d), lambda i, t: (i, jnp.maximum(t * hb - 1, 0), 0)),
            pl.BlockSpec((1, tm, d), lambda i, t: (i, t, 0)),
            pl.BlockSpec((1, HALO, d), lambda i, t: (i, jnp.minimum((t + 1) * hb, l // HALO - 1), 0)),
            pl.BlockSpec((1, d), const),
            pl.BlockSpec((d, nz), const),
            pl.BlockSpec((d, nx), const),
            pl.BlockSpec((d, ndt), const),
            pl.BlockSpec((8, nx), const),
            pl.BlockSpec((1, nx), const),
        ],
        out_specs=[
            pl.BlockSpec((1, tm, nz), lambda i, t: (i, t, 0)),
            pl.BlockSpec((1, tm, nx), lambda i, t: (i, t, 0)),
            pl.BlockSpec((1, tm, ndt), lambda i, t: (i, t, 0)),
        ],
        out_shape=[
            jax.ShapeDtypeStruct((b, l, nz), F32),
            jax.ShapeDtypeStruct((b, l, nx), F32),
            jax.ShapeDtypeStruct((b, l, ndt), F32),
        ],
        scratch_shapes=[
            pltpu.VMEM((tm, d), BF16),
            pltpu.VMEM((d // LANES, (SUBLANES + 2) * (seg + SUBLANES), LANES), F32),
            pltpu.VMEM(((seg + 2 * CONV_PAD_TILES) * SUBLANES, d), BF16),
            pltpu.VMEM((2, (seg + 2 * CONV_PAD_TILES) * SUBLANES, CONV_CB), F32),
            pltpu.VMEM((CONV_CB // LANES, tm, LANES), F32),
        ],
        compiler_params=pltpu.CompilerParams(
            dimension_semantics=("arbitrary", "arbitrary"), vmem_limit_bytes=VMEM_LIMIT),
        name="ssd_in",
    )(x, x, x, g, wz, wx, wdt, cw, cb)


def _ssd_decay_inputs(dt_ref, bias_ref, alog_ref):
    dtv = _softplus(dt_ref[0] + bias_ref[...])
    la = dtv * (-jnp.exp(alog_ref[...]))
    return dtv, la


def _chunk_rows(x, q, idx):
    tb, n = x.shape
    return jnp.concatenate(
        [jnp.broadcast_to(x[c * q + idx:c * q + idx + 1, :], (q, n)) for c in range(tb // q)], axis=0)


def _ssd_bwd_kernel(xs_ref, b_ref, c_ref, dt_ref, bias_ref, alog_ref, selb_ref, yb_ref, s_scr, *, tb):
    q = SSD_Q

    @pl.when(pl.program_id(1) == 0)
    def _():
        s_scr[...] = jnp.zeros_like(s_scr)

    dtv, la = _ssd_decay_inputs(dt_ref, bias_ref, alog_ref)
    rcs = _cumsum(_tri_blocks(tb, q, True).astype(BF16), la)
    eexp = _expand(jnp.exp(rcs), selb_ref)
    w = dtv * jnp.exp(_chunk_rows(rcs, q, 0) - rcs)
    xw = (xs_ref[0] * _expand(w, selb_ref)).astype(BF16)
    gw = SSD_R * SSD_P
    for c in reversed(range(tb // q)):
        sl = slice(c * q, (c + 1) * q)
        for g in range(SSD_G):
            ns = slice(g * SSD_N, (g + 1) * SSD_N)
            ps = slice(g * gw, (g + 1) * gw)
            s_in = s_scr[g]
            yb_ref[0, sl, ps] = _dot(c_ref[0, sl, ns].astype(BF16), s_in.astype(BF16)) * eexp[sl, ps]
            s_scr[g] = s_in * eexp[c * q:c * q + 1, ps] + _dot_tn(b_ref[0, sl, ns].astype(BF16), xw[sl, ps])


def _ssd_fwd_kernel(xs_ref, b_ref, c_ref, dt_ref, z_ref, yb_ref, xres_ref, bias_ref, alog_ref, self_ref,
                    dexp_ref, ng_ref, wout_ref, out_ref, s_scr, y_scr, *, tb):
    q = SSD_Q

    @pl.when(pl.program_id(1) == 0)
    def _():
        s_scr[...] = jnp.zeros_like(s_scr)

    dtv, la = _ssd_decay_inputs(dt_ref, bias_ref, alog_ref)
    cs = _cumsum(_tri_blocks(tb, q, False).astype(BF16), la)
    rcs = _cumsum(_tri_blocks(tb, q, True).astype(BF16), la)
    lower = _tri(q, False)
    upper = _tri(q, True)
    ml = lower.astype(F32)
    mu = upper.astype(F32)
    lane = lax.broadcasted_iota(jnp.int32, (q, 2 * SSD_P), 1)
    gw = SSD_R * SSD_P
    for c in range(tb // q):
        sl = slice(c * q, (c + 1) * q)
        csc, rcsc = cs[sl], rcs[sl]
        cs_t, rcs_t, dt_t = csc.T, rcsc.T, dtv[sl].T
        for g in range(SSD_G):
            ns = slice(g * SSD_N, (g + 1) * SSD_N)
            gm = _dot_nt(c_ref[0, sl, ns].astype(BF16), b_ref[0, sl, ns].astype(BF16))
            for pair in range(SSD_R // 2):
                col = g * gw + pair * 2 * SSD_P
                xpair = xs_ref[0, sl, col:col + 2 * SSD_P]
                acc = jnp.zeros((q, 2 * SSD_P), F32)
                for half in range(2):
                    h = g * SSD_R + pair * 2 + half
                    hb = SSD_H + h
                    seg = jnp.where(lower, csc[:, h:h + 1] - cs_t[h:h + 1, :], rcsc[:, hb:hb + 1] - rcs_t[hb:hb + 1, :])
                    wt = ml * dt_t[h:h + 1, :] + mu * dt_t[hb:hb + 1, :]
                    m = (gm * jnp.exp(seg) * wt).astype(BF16)
                    keep = (lane < SSD_P) if half == 0 else (lane >= SSD_P)
                    acc = acc + _dot(m, jnp.where(keep, xpair, 0.0).astype(BF16))
                y_scr[sl, col:col + 2 * SSD_P] = acc
    eexp = _expand(jnp.exp(cs), self_ref)
    w = dtv * jnp.exp(_chunk_rows(cs, q, q - 1) - cs)
    xs = xs_ref[0]
    xw = (xs * _expand(w, self_ref)).astype(BF16)
    for c in range(tb // q):
        sl = slice(c * q, (c + 1) * q)
        for g in range(SSD_G):
            ns = slice(g * SSD_N, (g + 1) * SSD_N)
            ps = slice(g * gw, (g + 1) * gw)
            s_in = s_scr[g]
            y_scr[sl, ps] = y_scr[sl, ps] + _dot(c_ref[0, sl, ns].astype(BF16), s_in.astype(BF16)) * eexp[sl, ps]
            s_scr[g] = (s_in * eexp[c * q + q - 1:c * q + q, ps]
                        + _dot_tn(b_ref[0, sl, ns].astype(BF16), xw[sl, ps]))
    y = y_scr[...] + yb_ref[0] + dexp_ref[...] * xs
    y = y * _silu(z_ref[0])
    yn = _rms(y, ng_ref[...]).astype(BF16)
    out_ref[0] = xres_ref[0] + _dot(yn, wout_ref[...])


def _ssd_sweeps(xbc, dt, z, xres, bias, alog, sel_f, sel_b, dexp, ng, wout, *, tb):
    b, l, _ = xbc.shape
    nblk = l // tb
    d = xres.shape[-1]
    const = lambda i, c: (0, 0)
    cp = pltpu.CompilerParams(dimension_semantics=("arbitrary", "arbitrary"), vmem_limit_bytes=VMEM_LIMIT)
    nb = SSD_DI // SSD_GN

    def block_specs(cmap):
        return [
            pl.BlockSpec((1, tb, SSD_DI), lambda i, c: (i, cmap(c), 0)),
            pl.BlockSpec((1, tb, SSD_GN), lambda i, c: (i, cmap(c), nb)),
            pl.BlockSpec((1, tb, SSD_GN), lambda i, c: (i, cmap(c), nb + 1)),
            pl.BlockSpec((1, tb, LANES), lambda i, c: (i, cmap(c), 0)),
        ]

    rev = lambda c: nblk - 1 - c
    fwd = lambda c: c
    state = pltpu.VMEM((SSD_G, SSD_N, SSD_R * SSD_P), F32)
    yb = pl.pallas_call(
        functools.partial(_ssd_bwd_kernel, tb=tb),
        grid=(b, nblk),
        in_specs=block_specs(rev) + [
            pl.BlockSpec((1, LANES), const),
            pl.BlockSpec((1, LANES), const),
            pl.BlockSpec((2 * LANES, SSD_DI), const),
        ],
        out_specs=pl.BlockSpec((1, tb, SSD_DI), lambda i, c: (i, rev(c), 0)),
        out_shape=jax.ShapeDtypeStruct((b, l, SSD_DI), F32),
        scratch_shapes=[state],
        compiler_params=cp,
        name="ssd_bwd",
    )(xbc, xbc, xbc, dt, bias, alog, sel_b)
    tok = lambda n: pl.BlockSpec((1, tb, n), lambda i, c: (i, c, 0))
    return pl.pallas_call(
        functools.partial(_ssd_fwd_kernel, tb=tb),
        grid=(b, nblk),
        in_specs=block_specs(fwd) + [
            tok(SSD_DI), tok(SSD_DI), tok(d),
            pl.BlockSpec((1, LANES), const),
            pl.BlockSpec((1, LANES), const),
            pl.BlockSpec((2 * LANES, SSD_DI), const),
            pl.BlockSpec((1, SSD_DI), const),
            pl.BlockSpec((1, SSD_DI), const),
            pl.BlockSpec((SSD_DI, d), const),
        ],
        out_specs=tok(d),
        out_shape=jax.ShapeDtypeStruct((b, l, d), F32),
        scratch_shapes=[state, pltpu.VMEM((tb, SSD_DI), F32)],
        compiler_params=cp,
        name="ssd_fwd",
    )(xbc, xbc, xbc, dt, z, yb, xres, bias, alog, sel_f, dexp, ng, wout)


def _gla_in_kernel(x_ref, g_ref, wqk_ref, wv_ref, wg_ref, wlr_ref, qk_ref, v_ref, go_ref, lr_ref):
    xn = _rms(x_ref[...], g_ref[...]).astype(BF16)
    qk_ref[...] = _dot(xn, wqk_ref[...])
    v_ref[...] = _dot(xn, wv_ref[...])
    go_ref[...] = _dot(xn, wg_ref[...])
    lr_ref[...] = _dot(xn, wlr_ref[...])


def _gla_in(x2, g, wqk, wv, wg, wlr, *, tm):
    t, d = x2.shape
    const = lambda i: (0, 0)
    widths = [wqk.shape[1], wv.shape[1], wg.shape[1], wlr.shape[1]]
    return pl.pallas_call(
        _gla_in_kernel,
        grid=(t // tm,),
        in_specs=[pl.BlockSpec((tm, d), lambda i: (i, 0)), pl.BlockSpec((1, d), const)]
        + [pl.BlockSpec((d, n), const) for n in widths],
        out_specs=[pl.BlockSpec((tm, n), lambda i: (i, 0)) for n in widths],
        out_shape=[jax.ShapeDtypeStruct((t, n), F32) for n in widths],
        compiler_params=pltpu.CompilerParams(dimension_semantics=("arbitrary",), vmem_limit_bytes=VMEM_LIMIT),
        name="gla_in",
    )(x2, g, wqk, wv, wg, wlr)


def _gla_log_gate(lr_bf16, gup_ref, gbias_ref):
    xg = _dot(lr_bf16, gup_ref[...]) + gbias_ref[...]
    return -_softplus(-xg) * (1.0 / GLA_GATE_NORMALIZER)


def _gla_bwd_kernel(qk_ref, v_ref, lr_ref, gup_ref, gbias_ref, ob_ref, s_scr, *, tb):
    q = GLA_Q

    @pl.when(pl.program_id(1) == 0)
    def _():
        s_scr[...] = jnp.zeros_like(s_scr)

    lg = _gla_log_gate(lr_ref[0].astype(BF16), gup_ref, gbias_ref)
    rb = _cumsum(_tri_blocks(tb, q, True).astype(BF16), lg)
    e = jnp.exp(rb)
    qb = (qk_ref[0, :, 0:GLA_KD] * (GLA_DK ** -0.5) * e).astype(BF16)
    kend = (qk_ref[0, :, GLA_KD:2 * GLA_KD] * jnp.exp(_chunk_rows(rb, q, 0) - rb)).astype(BF16)
    for c in reversed(range(tb // q)):
        sl = slice(c * q, (c + 1) * q)
        for h in range(GLA_H):
            ksl = slice(h * GLA_DK, (h + 1) * GLA_DK)
            vsl = slice(h * GLA_DV, (h + 1) * GLA_DV)
            s_in = s_scr[h]
            ob_ref[0, sl, vsl] = _dot_nt(qb[sl, ksl], s_in.astype(BF16))
            s_scr[h] = s_in * e[c * q:c * q + 1, ksl] + _dot_tn(v_ref[0, sl, vsl].astype(BF16), kend[sl, ksl])


def _gla_fwd_kernel(qk_ref, v_ref, lr_ref, go_ref, ob_ref, xres_ref, gupf_ref, gbf_ref, gupb_ref, gbb_ref,
                    ng_ref, wout_ref, out_ref, s_scr, o_scr, *, tb):
    q = GLA_Q

    @pl.when(pl.program_id(1) == 0)
    def _():
        s_scr[...] = jnp.zeros_like(s_scr)

    lr = lr_ref[0].astype(BF16)
    lower = _tri_blocks(tb, q, False)
    upper = _tri_blocks(tb, q, True)
    bf = _cumsum(lower.astype(BF16), _gla_log_gate(lr, gupf_ref, gbf_ref))
    rb = _cumsum(upper.astype(BF16), _gla_log_gate(lr, gupb_ref, gbb_ref))
    qs = qk_ref[0, :, 0:GLA_KD] * (GLA_DK ** -0.5)
    ks = qk_ref[0, :, GLA_KD:2 * GLA_KD]
    mid_f = _chunk_rows(bf, q, q // 2)
    mid_b = _chunk_rows(rb, q, q - 1 - q // 2)
    qin_f = (qs * jnp.exp(bf - mid_f)).astype(BF16)
    kin_f = (ks * jnp.exp(mid_f - bf)).astype(BF16)
    qin_b = (qs * jnp.exp(rb - mid_b)).astype(BF16)
    kin_b = (ks * jnp.exp(mid_b - rb)).astype(BF16)
    e = jnp.exp(bf)
    qb = (qs * e).astype(BF16)
    kend = (ks * jnp.exp(_chunk_rows(bf, q, q - 1) - bf)).astype(BF16)
    for h in range(GLA_H):
        ksl = slice(h * GLA_DK, (h + 1) * GLA_DK)
        vsl = slice(h * GLA_DV, (h + 1) * GLA_DV)
        vh = v_ref[0, :, vsl].astype(BF16)
        attn = (jnp.where(lower, _dot_nt(qin_f[:, ksl], kin_f[:, ksl]), 0.0)
                + jnp.where(upper, _dot_nt(qin_b[:, ksl], kin_b[:, ksl]), 0.0))
        o_scr[:, vsl] = _dot(attn.astype(BF16), vh) + ob_ref[0, :, vsl]
        for c in range(tb // q):
            sl = slice(c * q, (c + 1) * q)
            s_in = s_scr[h]
            o_scr[sl, vsl] = o_scr[sl, vsl] + _dot_nt(qb[sl, ksl], s_in.astype(BF16))
            s_scr[h] = s_in * e[c * q + q - 1:c * q + q, ksl] + _dot_tn(vh[sl], kend[sl, ksl])
        o_scr[:, vsl] = _rms(o_scr[:, vsl], ng_ref[...])
    o = (o_scr[...] * _silu(go_ref[0])).astype(BF16)
    out_ref[0] = xres_ref[0] + _dot(o, wout_ref[...])


def _gla_sweeps(qk, v, lr, go, xres, gupf, gbf, gupb, gbb, ng, wout, *, tb):
    b, l, _ = qk.shape
    nblk = l // tb
    d = xres.shape[-1]
    const = lambda i, c: (0, 0)
    cp = pltpu.CompilerParams(dimension_semantics=("arbitrary", "arbitrary"), vmem_limit_bytes=VMEM_LIMIT)
    rev = lambda c: nblk - 1 - c
    state = pltpu.VMEM((GLA_H, GLA_DV, GLA_DK), F32)
    ob = pl.pallas_call(
        functools.partial(_gla_bwd_kernel, tb=tb),
        grid=(b, nblk),
        in_specs=[
            pl.BlockSpec((1, tb, 2 * GLA_KD), lambda i, c: (i, rev(c), 0)),
            pl.BlockSpec((1, tb, GLA_VD), lambda i, c: (i, rev(c), 0)),
            pl.BlockSpec((1, tb, LANES), lambda i, c: (i, rev(c), 0)),
            pl.BlockSpec((LANES, GLA_KD), const),
            pl.BlockSpec((1, GLA_KD), const),
        ],
        out_specs=pl.BlockSpec((1, tb, GLA_VD), lambda i, c: (i, rev(c), 0)),
        out_shape=jax.ShapeDtypeStruct((b, l, GLA_VD), F32),
        scratch_shapes=[state],
        compiler_params=cp,
        name="gla_bwd",
    )(qk, v, lr, gupb, gbb)
    tok = lambda n: pl.BlockSpec((1, tb, n), lambda i, c: (i, c, 0))
    return pl.pallas_call(
        functools.partial(_gla_fwd_kernel, tb=tb),
        grid=(b, nblk),
        in_specs=[tok(2 * GLA_KD), tok(GLA_VD), tok(LANES), tok(GLA_VD), tok(GLA_VD), tok(d),
                  pl.BlockSpec((LANES, GLA_KD), const), pl.BlockSpec((1, GLA_KD), const),
                  pl.BlockSpec((LANES, GLA_KD), const), pl.BlockSpec((1, GLA_KD), const),
                  pl.BlockSpec((1, GLA_DV), const), pl.BlockSpec((GLA_VD, d), const)],
        out_specs=tok(d),
        out_shape=jax.ShapeDtypeStruct((b, l, d), F32),
        scratch_shapes=[state, pltpu.VMEM((tb, GLA_VD), F32)],
        compiler_params=cp,
        name="gla_fwd",
    )(qk, v, lr, go, ob, xres, gupf, gbf, gupb, gbb, ng, wout)


def _mlp_kernel(*refs, fc, final):
    if final:
        x_ref, g_ref, wup_ref, wdn_ref, fg_ref, out_ref = refs
    else:
        x_ref, g_ref, wup_ref, wdn_ref, out_ref = refs
    x = x_ref[...]
    xn = _rms(x, g_ref[...]).astype(BF16)
    acc = x
    for j in range(wup_ref.shape[1] // fc):
        h = jnp.square(jnp.maximum(_dot(xn, wup_ref[:, j * fc:(j + 1) * fc]), 0.0)).astype(BF16)
        acc = acc + _dot(h, wdn_ref[j * fc:(j + 1) * fc, :])
    if final:
        acc = _rms(acc, fg_ref[...])
    out_ref[...] = acc


def _mlp(x2, g, wup, wdn, final_g, *, tm, fc):
    t, d = x2.shape
    ff = wup.shape[1]
    const = lambda i: (0, 0)
    final = final_g is not None
    ins = [x2, g, wup, wdn] + ([final_g] if final else [])
    specs = [pl.BlockSpec((tm, d), lambda i: (i, 0)), pl.BlockSpec((1, d), const),
             pl.BlockSpec((d, ff), const), pl.BlockSpec((ff, d), const)] + ([pl.BlockSpec((1, d), const)] if final else [])
    return pl.pallas_call(
        functools.partial(_mlp_kernel, fc=fc, final=final),
        grid=(t // tm,),
        in_specs=specs,
        out_specs=pl.BlockSpec((tm, d), lambda i: (i, 0)),
        out_shape=jax.ShapeDtypeStruct((t, d), F32),
        compiler_params=pltpu.CompilerParams(dimension_semantics=("arbitrary",), vmem_limit_bytes=VMEM_LIMIT),
        name="mlp_final" if final else "mlp",
    )(*ins)


def _pad_cols(w, n):
    return jnp.pad(w, ((0, 0), (0, n - w.shape[1])))


def _row(v, n=None):
    v = v.reshape(1, -1).astype(F32)
    return v if n is None else _pad_cols(v, n)


def _head_selector(first_row):
    r = jnp.arange(LANES)[:, None]
    c = jnp.arange(SSD_DI)[None, :]
    sel = (r == first_row + c // SSD_P).astype(BF16)
    return jnp.concatenate([sel, sel], axis=0)


def _ssd_layer(x, mix_g, p, *, tm):
    w_in = p["w_in"]
    wz = w_in[:, :SSD_DI].astype(BF16)
    wx = w_in[:, SSD_DI:2 * SSD_DI + 2 * SSD_GN].astype(BF16)
    wdt = _pad_cols(w_in[:, 2 * SSD_DI + 2 * SSD_GN:], LANES).astype(BF16)
    cw = jnp.pad(p["conv_w"].astype(F32), ((0, 8 - SSD_CONV_W), (0, 0)))
    z, xbc, dt = _ssd_in(x, _row(mix_g), wz, wx, wdt, cw, _row(p["conv_b"]), tm=tm)
    bias = _row(jnp.concatenate([p["dt_bias_f"], p["dt_bias_b"]]), LANES)
    alog = _row(jnp.concatenate([p["a_log_f"], p["a_log_b"]]), LANES)
    dexp = _row(jnp.repeat(p["d"], SSD_P))
    return _ssd_sweeps(xbc, dt, z, x, bias, alog, _head_selector(0), _head_selector(SSD_H), dexp,
                       _row(p["norm_g"]), p["w_out"].astype(BF16), tb=SSD_TB)


def _gla_layer(x, mix_g, p, *, tm):
    b, l, d = x.shape
    w_in = p["w_in"]
    o1, o2, o3 = 2 * GLA_KD, 2 * GLA_KD + GLA_VD, 2 * GLA_KD + 2 * GLA_VD
    qk, v, go, lr = _gla_in(x.reshape(b * l, d), _row(mix_g), w_in[:, :o1].astype(BF16), w_in[:, o1:o2].astype(BF16),
                            w_in[:, o2:o3].astype(BF16), _pad_cols(w_in[:, o3:], LANES).astype(BF16), tm=tm)
    pad_rows = lambda w, first: jnp.pad(w, ((first, LANES - first - GLA_RANK), (0, 0))).astype(BF16)
    sh = lambda a: a.reshape(b, l, a.shape[-1])
    return _gla_sweeps(sh(qk), sh(v), sh(lr), sh(go), x,
                       pad_rows(p["gate_up_f"], 0), _row(p["gate_bias_f"]),
                       pad_rows(p["gate_up_b"], GLA_RANK), _row(p["gate_bias_b"]),
                       _row(p["norm_g"]), p["w_out"].astype(BF16), tb=GLA_TB)


def _trunk(x, norm_mix_g, norm_mlp_g, norm_final_g, ssd, gla, mlp_w_up, mlp_w_down):
    b, l, d = x.shape
    depth = norm_mix_g.shape[0]
    tm = min(512, l)
    for i in range(depth):
        j = i // 2
        if i % 2 == 0:
            x = _ssd_layer(x, norm_mix_g[i], {k: v[j] for k, v in ssd.items()}, tm=tm)
        else:
            x = _gla_layer(x, norm_mix_g[i], {k: v[j] for k, v in gla.items()}, tm=tm)
        final_g = _row(norm_final_g) if i == depth - 1 else None
        x = _mlp(x.reshape(b * l, d), _row(norm_mlp_g[i]), mlp_w_up[i].astype(BF16), mlp_w_down[i].astype(BF16),
                 final_g, tm=tm, fc=1024).reshape(b, l, d)
    return x


def kernel(x_prompt, x_sample, norm_mix_g, norm_mlp_g, norm_final_g, ssd_w_in, ssd_conv_w, ssd_conv_b, ssd_dt_bias_f, ssd_dt_bias_b, ssd_a_log_f, ssd_a_log_b, ssd_d, ssd_norm_g, ssd_w_out, gla_w_in, gla_gate_up_f, gla_gate_bias_f, gla_gate_up_b, gla_gate_bias_b, gla_norm_g, gla_w_out, mlp_w_up, mlp_w_down):
    ssd = dict(w_in=ssd_w_in, conv_w=ssd_conv_w, conv_b=ssd_conv_b, dt_bias_f=ssd_dt_bias_f, dt_bias_b=ssd_dt_bias_b,
               a_log_f=ssd_a_log_f, a_log_b=ssd_a_log_b, d=ssd_d, norm_g=ssd_norm_g, w_out=ssd_w_out)
    gla = dict(w_in=gla_w_in, gate_up_f=gla_gate_up_f, gate_bias_f=gla_gate_bias_f, gate_up_b=gla_gate_up_b,
               gate_bias_b=gla_gate_bias_b, norm_g=gla_norm_g, w_out=gla_w_out)
    run = functools.partial(_trunk, norm_mix_g=norm_mix_g, norm_mlp_g=norm_mlp_g, norm_final_g=norm_final_g,
                            ssd=ssd, gla=gla, mlp_w_up=mlp_w_up, mlp_w_down=mlp_w_down)
    return (run(x_prompt), run(x_sample))
```
